```python
import math
import jax, jax.numpy as jnp
from jax import lax
import numpy as np

D_MODEL = 2048
BATCH = 8
SEQ = 2048
DEPTH = 1

ATTN_WIDTH = D_MODEL // 2
CONV_WIDTH = D_MODEL - ATTN_WIDTH
HEAD_DIM = 128
N_ATTN_HEADS = ATTN_WIDTH // HEAD_DIM
CONV_GROUPS = 8
CONV_K = 3
MOBA_BLOCK = 256
MOBA_TOPK = 3
Q_CHUNK = 64
D_FF = 4 * D_MODEL
LN_EPS = 1e-5
RMS_EPS = 1e-6
DEEPNORM_ALPHA = (2.0 * DEPTH) ** 0.25
DEEPNORM_BETA = (8.0 * DEPTH) ** -0.25

kernel_name = 'hymba_moba_shortconv_deepnorm_block'


def layer_norm(x, g, b):
    xf = x.astype(jnp.float32)
    mu = jnp.mean(xf, axis=-1, keepdims=True)
    var = jnp.mean(jnp.square(xf - mu), axis=-1, keepdims=True)
    return ((xf - mu) * lax.rsqrt(var + LN_EPS) * g.astype(jnp.float32) + b.astype(jnp.float32)).astype(x.dtype)


def rms_norm(x, g):
    xf = x.astype(jnp.float32)
    ms = jnp.mean(jnp.square(xf), axis=-1, keepdims=True)
    return (xf * lax.rsqrt(ms + RMS_EPS) * g.astype(jnp.float32)).astype(x.dtype)


def moba_attention(q, k, v):
    B, S, H, Dh = q.shape
    n_blocks = -(-S // MOBA_BLOCK)
    s_pad = n_blocks * MOBA_BLOCK
    top_k = max(1, min(MOBA_TOPK, n_blocks - 1))
    n_chunks = S // Q_CHUNK
    scale = Dh ** -0.5

    pad = ((0, 0), (0, s_pad - S), (0, 0), (0, 0))
    kb = jnp.pad(k, pad).reshape(B, n_blocks, MOBA_BLOCK, H, Dh).transpose(0, 3, 1, 2, 4)
    vb = jnp.pad(v, pad).reshape(B, n_blocks, MOBA_BLOCK, H, Dh).transpose(0, 3, 1, 2, 4)
    k_mean = jnp.mean(kb.astype(jnp.float32), axis=3)
    qc = q.reshape(B, n_chunks, Q_CHUNK, H, Dh).transpose(0, 1, 3, 2, 4)

    chunk_ids = jnp.arange(n_chunks)
    head_ids = jnp.arange(H)[:, None, None]
    blk_ids = jnp.arange(n_blocks)
    key_off = jnp.arange(MOBA_BLOCK)
    q_off = jnp.arange(Q_CHUNK)
    slot_ids = jnp.arange(top_k)

    def per_batch(args):
        q_b, kb_b, vb_b, km_b = args

        def per_chunk(cargs):
            q_c, c = cargs
            q_pos = c * Q_CHUNK + q_off
            own = (c * Q_CHUNK) // MOBA_BLOCK
            gate = jnp.einsum('hqd,hnd->hqn', q_c.astype(jnp.float32), km_b)
            gate = jnp.where(blk_ids[None, None, :] < own, gate, -jnp.inf)
            _, sel = lax.top_k(gate, top_k)
            sel_valid = slot_ids < own
            k_sel = kb_b[head_ids, sel]
            v_sel = vb_b[head_ids, sel]
            s_sel = jnp.einsum('hqd,hqkjd->hqkj', q_c, k_sel,
                               preferred_element_type=jnp.float32) * scale
            s_sel = jnp.where(sel_valid[None, None, :, None], s_sel, -jnp.inf)
            k_own = lax.dynamic_index_in_dim(kb_b, own, axis=1, keepdims=False)
            v_own = lax.dynamic_index_in_dim(vb_b, own, axis=1, keepdims=False)
            s_own = jnp.einsum('hqd,hjd->hqj', q_c, k_own,
                               preferred_element_type=jnp.float32) * scale
            k_pos = own * MOBA_BLOCK + key_off
            s_own = jnp.where(k_pos[None, None, :] <= q_pos[None, :, None], s_own, -jnp.inf)
            s_all = jnp.concatenate([s_sel.reshape(H, Q_CHUNK, top_k * MOBA_BLOCK), s_own], axis=-1)
            p = jax.nn.softmax(s_all, axis=-1).astype(v_sel.dtype)
            p_sel = p[..., :top_k * MOBA_BLOCK].reshape(H, Q_CHUNK, top_k, MOBA_BLOCK)
            p_own = p[..., top_k * MOBA_BLOCK:]
            return (jnp.einsum('hqkj,hqkjd->hqd', p_sel, v_sel)
                    + jnp.einsum('hqj,hjd->hqd', p_own, v_own))

        return lax.map(per_chunk, (q_b, chunk_ids))

    out = lax.map(per_batch, (qc, kb, vb, k_mean))
    return out.transpose(0, 1, 3, 2, 4).reshape(B, S, H * Dh)


def short_conv(b_gate, c_gate, u, conv_w):
    S = u.shape[1]
    z = c_gate * u
    zp = jnp.pad(z, ((0, 0), (CONV_K - 1, 0), (0, 0)))
    y = conv_w[0] * zp[:, 0:S]
    for j in range(1, CONV_K):
        y = y + conv_w[j] * zp[:, j:j + S]
    return b_gate * y


def setup_inputs(seed: int = 0) -> dict:
    key = jax.random.key(seed)
    ks = jax.random.split(key, 13)
    d = D_MODEL
    n_in = 3 * ATTN_WIDTH + 3 * CONV_WIDTH
    nrm = jax.random.normal
    f32 = jnp.float32
    x = nrm(ks[0], (BATCH, SEQ, d), f32)
    w_in = nrm(ks[1], (DEPTH, d, n_in), f32) * d ** -0.5
    conv_w = nrm(ks[2], (DEPTH, CONV_K, CONV_WIDTH), f32) * CONV_K ** -0.5
    attn_out_g = 1.0 + 0.02 * nrm(ks[3], (DEPTH, ATTN_WIDTH), f32)
    conv_out_g = 1.0 + 0.02 * nrm(ks[4], (DEPTH, CONV_WIDTH), f32)
    w_o = nrm(ks[5], (DEPTH, d, d), f32) * (d ** -0.5 * DEEPNORM_BETA)
    ln1_g = 1.0 + 0.02 * nrm(ks[6], (DEPTH, d), f32)
    ln1_b = 0.02 * nrm(ks[7], (DEPTH, d), f32)
    w_up = nrm(ks[8], (DEPTH, d, D_FF), f32) * d ** -0.5
    w_down = nrm(ks[9], (DEPTH, D_FF, d), f32) * (D_FF ** -0.5 * DEEPNORM_BETA)
    ln2_g = 1.0 + 0.02 * nrm(ks[10], (DEPTH, d), f32)
    ln2_b = 0.02 * nrm(ks[11], (DEPTH, d), f32)
    return {'x': x, 'w_in': w_in, 'conv_w': conv_w, 'attn_out_g': attn_out_g,
            'conv_out_g': conv_out_g, 'w_o': w_o, 'ln1_g': ln1_g, 'ln1_b': ln1_b,
            'w_up': w_up, 'w_down': w_down, 'ln2_g': ln2_g, 'ln2_b': ln2_b}


def reference(x, w_in, conv_w, attn_out_g, conv_out_g, w_o, ln1_g, ln1_b,
              w_up, w_down, ln2_g, ln2_b):
    B, S, _ = x.shape
    A, C = ATTN_WIDTH, CONV_WIDTH
    splits = [A, 2 * A, 3 * A, 3 * A + C, 3 * A + 2 * C]
    for l in range(DEPTH):
        proj = x @ w_in[l]
        q, k, v, b_gate, c_gate, u = jnp.split(proj, splits, axis=-1)
        q = q.reshape(B, S, N_ATTN_HEADS, HEAD_DIM)
        k = k.reshape(B, S, N_ATTN_HEADS, HEAD_DIM)
        v = v.reshape(B, S, N_ATTN_HEADS, HEAD_DIM)
        attn = moba_attention(q, k, v)
        conv = short_conv(b_gate, c_gate, u, conv_w[l])
        mixed = jnp.concatenate([rms_norm(attn, attn_out_g[l]),
                                 rms_norm(conv, conv_out_g[l])], axis=-1)
        h = layer_norm(DEEPNORM_ALPHA * x + mixed @ w_o[l], ln1_g[l], ln1_b[l])
        ff = jnp.square(jax.nn.relu(h @ w_up[l])) @ w_down[l]
        x = layer_norm(DEEPNORM_ALPHA * h + ff, ln2_g[l], ln2_b[l])
    return x
```

```python
import functools

import jax
import jax.numpy as jnp
from jax import lax
from jax.experimental import pallas as pl
from jax.experimental.pallas import tpu as pltpu

ATTN_WIDTH = 1024
CONV_WIDTH = 1024
HEAD_DIM = 128
N_HEADS = ATTN_WIDTH // HEAD_DIM
CONV_K = 3
MOBA_BLOCK = 256
MOBA_TOPK = 3
LN_EPS = 1e-5
RMS_EPS = 1e-6
DEPTH = 1
DEEPNORM_ALPHA = (2.0 * DEPTH) ** 0.25

SUBLANES = 8
VMEM_LIMIT_BYTES = 56 * 1024 * 1024

F32 = jnp.float32
BF16 = jnp.bfloat16


def _compiler_params(semantics):
    return pltpu.CompilerParams(dimension_semantics=semantics,
                                vmem_limit_bytes=VMEM_LIMIT_BYTES)


def _proj_kernel(x_ref, w_ref, s_ref, o_ref):
    acc = jnp.dot(x_ref[...], w_ref[...], preferred_element_type=F32)
    o_ref[...] = (acc * s_ref[...]).astype(o_ref.dtype)


def _proj(xb, wb, col_scale, tm=1024, tn=1024):
    t, d = xb.shape
    n = wb.shape[1]
    return pl.pallas_call(
        _proj_kernel,
        grid=(t // tm, n // tn),
        in_specs=[pl.BlockSpec((tm, d), lambda i, j: (i, 0)),
                  pl.BlockSpec((d, tn), lambda i, j: (0, j)),
                  pl.BlockSpec((1, tn), lambda i, j: (0, j))],
        out_specs=pl.BlockSpec((tm, tn), lambda i, j: (i, j)),
        out_shape=jax.ShapeDtypeStruct((t, n), BF16),
        compiler_params=_compiler_params(("parallel", "arbitrary")),
        name="proj",
    )(xb, wb, col_scale)


def _attn_kernel(q_ref, k_ref, v_ref, o_ref, *, seq, n_blocks, top_k):
    blk = MOBA_BLOCK
    k_all = k_ref[...]
    km = jnp.mean(k_all.astype(F32).reshape(n_blocks, blk, HEAD_DIM), axis=1)
    km_hi = km.astype(BF16)
    km_lo = (km - km_hi.astype(F32)).astype(BF16)
    q_all = q_ref[...]
    nt = (((1,), (1,)), ((), ()))
    gate = (lax.dot_general(km_hi, q_all, nt, preferred_element_type=F32)
            + lax.dot_general(km_lo, q_all, nt, preferred_element_type=F32))

    blk_id = lax.broadcasted_iota(jnp.int32, (n_blocks, seq), 0)
    own = lax.broadcasted_iota(jnp.int32, (n_blocks, seq), 1) // blk
    rank = jnp.zeros((n_blocks, seq), jnp.int32)
    for m in range(n_blocks - 1):
        g_m = gate[m:m + 1, :]
        beats = (g_m > gate) | ((g_m == gate) & (m < blk_id))
        rank = rank + jnp.where(beats & (m < own), 1, 0)
    sel = (rank < top_k) & (blk_id < own)

    vt = v_ref[...].astype(F32).T.astype(BF16)

    key_in_blk = lax.broadcasted_iota(jnp.int32, (blk, blk), 0)
    qry_in_blk = lax.broadcasted_iota(jnp.int32, (blk, blk), 1)
    causal = key_in_blk <= qry_in_blk
    neg_inf = jnp.float32(-jnp.inf)

    for i in range(n_blocks):
        lo, hi = i * blk, (i + 1) * blk
        q_i = q_ref[lo:hi, :]
        s = lax.dot_general(k_ref[0:hi, :], q_i, nt,
                            preferred_element_type=F32)
        pieces = [jnp.where(sel[n:n + 1, lo:hi], s[n * blk:(n + 1) * blk, :], neg_inf)
                  for n in range(i)]
        pieces.append(jnp.where(causal, s[lo:hi, :], neg_inf))
        mx = functools.reduce(jnp.maximum, pieces)
        m_row = jnp.max(mx, axis=0, keepdims=True)
        probs = [jnp.exp(p - m_row) for p in pieces]
        denom = jnp.sum(functools.reduce(jnp.add, probs), axis=0, keepdims=True)
        p_all = jnp.concatenate([p.astype(BF16) for p in probs], axis=0)
        o_t = jnp.dot(vt[:, 0:hi], p_all, preferred_element_type=F32)
        o_ref[lo:hi, :] = (o_t / denom).T.astype(o_ref.dtype)


def _attention(proj, batch, seq):
    n_blocks = seq // MOBA_BLOCK
    top_k = max(1, min(MOBA_TOPK, n_blocks - 1))
    kern = functools.partial(_attn_kernel, seq=seq, n_blocks=n_blocks, top_k=top_k)
    head_spec = lambda off: pl.BlockSpec((seq, HEAD_DIM), lambda b, h: (b, off + h))
    return pl.pallas_call(
        kern,
        grid=(batch, N_HEADS),
        in_specs=[head_spec(0), head_spec(N_HEADS), head_spec(2 * N_HEADS)],
        out_specs=pl.BlockSpec((seq, HEAD_DIM), lambda b, h: (b, h)),
        out_shape=jax.ShapeDtypeStruct((batch * seq, ATTN_WIDTH), BF16),
        compiler_params=_compiler_params(("parallel", "arbitrary")),
        name="moba_attn",
    )(proj, proj, proj)


def _rms_norm(v, gain):
    ms = jnp.mean(jnp.square(v), axis=-1, keepdims=True)
    return v * lax.rsqrt(ms + RMS_EPS) * gain


def _layer_norm(v, gain, bias):
    mu = jnp.mean(v, axis=-1, keepdims=True)
    vc = v - mu
    var = jnp.mean(jnp.square(vc), axis=-1, keepdims=True)
    return vc * lax.rsqrt(var + LN_EPS) * gain + bias


def _mix_kernel(attn_ref, b_ref, c_ref, u_ref, ch_ref, uh_ref, x_ref, cw_ref,
                ga_ref, gc_ref, wo_ref, g1_ref, b1_ref, h_ref, hb_ref, *, tiles_per_seq):
    i = pl.program_id(0)
    tm = attn_ref.shape[0]
    ra = _rms_norm(attn_ref[...].astype(F32), ga_ref[...])

    z = c_ref[...].astype(F32) * u_ref[...].astype(F32)
    not_first = (i % tiles_per_seq != 0).astype(F32)
    zh = ch_ref[...].astype(F32) * uh_ref[...].astype(F32) * not_first
    row = lax.broadcasted_iota(jnp.int32, z.shape, 0)
    z1 = jnp.where(row == 0, zh[SUBLANES - 1:SUBLANES, :], pltpu.roll(z, 1, 0))
    z2 = jnp.where(row == 0, zh[SUBLANES - 2:SUBLANES - 1, :],
                   jnp.where(row == 1, zh[SUBLANES - 1:SUBLANES, :], pltpu.roll(z, 2, 0)))
    y = cw_ref[0:1, :] * z2 + cw_ref[1:2, :] * z1 + cw_ref[2:3, :] * z
    rc = _rms_norm(b_ref[...].astype(F32) * y, gc_ref[...])

    a_w = attn_ref.shape[1]
    m = (jnp.dot(ra.astype(BF16), wo_ref[0:a_w, :], preferred_element_type=F32)
         + jnp.dot(rc.astype(BF16), wo_ref[a_w:, :], preferred_element_type=F32))
    h = _layer_norm(DEEPNORM_ALPHA * x_ref[...] + m, g1_ref[...], b1_ref[...])
    h_ref[...] = h
    hb_ref[...] = h.astype(hb_ref.dtype)


def _mix(attn, proj, x2, conv_w, ga, gc, wo_b, g1, b1, seq, tm=256):
    t, d = x2.shape
    a_w, c_w = ATTN_WIDTH, CONV_WIDTH
    tiles_per_seq = seq // tm
    halo_blocks = tm // SUBLANES
    cb = 3 * a_w // c_w
    row_blk = lambda col: pl.BlockSpec((tm, c_w), lambda i: (i, col))
    halo = lambda col: pl.BlockSpec(
        (SUBLANES, c_w), lambda i: (jnp.maximum(i * halo_blocks - 1, 0), col))
    vec = lambda w: pl.BlockSpec((1, w), lambda i: (0, 0))
    kern = functools.partial(_mix_kernel, tiles_per_seq=tiles_per_seq)
    return pl.pallas_call(
        kern,
        grid=(t // tm,),
        in_specs=[pl.BlockSpec((tm, a_w), lambda i: (i, 0)),
                  row_blk(cb), row_blk(cb + 1), row_blk(cb + 2),
                  halo(cb + 1), halo(cb + 2),
                  pl.BlockSpec((tm, d), lambda i: (i, 0)),
                  pl.BlockSpec((CONV_K, c_w), lambda i: (0, 0)),
                  vec(a_w), vec(c_w),
                  pl.BlockSpec((d, d), lambda i: (0, 0)),
                  vec(d), vec(d)],
        out_specs=[pl.BlockSpec((tm, d), lambda i: (i, 0)),
                   pl.BlockSpec((tm, d), lambda i: (i, 0))],
        out_shape=[jax.ShapeDtypeStruct((t, d), F32),
                   jax.ShapeDtypeStruct((t, d), BF16)],
        compiler_params=_compiler_params(("parallel",)),
        name="mix_ln1",
    )(attn, proj, proj, proj, proj, proj, x2, conv_w, ga, gc, wo_b, g1, b1)


def _ffn_kernel(h_ref, hb_ref, wu_ref, wd_ref, g_ref, b_ref, o_ref):
    f = pl.program_id(1)
    up = jnp.dot(hb_ref[...], wu_ref[...], preferred_element_type=F32)
    act = jnp.square(jnp.maximum(up, 0.0)).astype(BF16)
    down = jnp.dot(act, wd_ref[...], preferred_element_type=F32)

    @pl.when(f == 0)
    def _():
        o_ref[...] = down

    @pl.when(f > 0)
    def _():
        o_ref[...] += down

    @pl.when(f == pl.num_programs(1) - 1)
    def _():
        o_ref[...] = _layer_norm(DEEPNORM_ALPHA * h_ref[...] + o_ref[...],
                                 g_ref[...], b_ref[...])


def _ffn(h, hb, wu_b, wd_b, g2, b2, tm=512, tf=512):
    t, d = h.shape
    d_ff = wu_b.shape[1]
    vec = pl.BlockSpec((1, d), lambda i, f: (0, 0))
    return pl.pallas_call(
        _ffn_kernel,
        grid=(t // tm, d_ff // tf),
        in_specs=[pl.BlockSpec((tm, d), lambda i, f: (i, 0)),
                  pl.BlockSpec((tm, d), lambda i, f: (i, 0)),
                  pl.BlockSpec((d, tf), lambda i, f: (0, f)),
                  pl.BlockSpec((tf, d), lambda i, f: (f, 0)),
                  vec, vec],
        out_specs=pl.BlockSpec((tm, d), lambda i, f: (i, 0)),
        out_shape=jax.ShapeDtypeStruct((t, d), F32),
        compiler_params=_compiler_params(("parallel", "arbitrary")),
        name="ffn_ln2",
    )(h, hb, wu_b, wd_b, g2, b2)


def kernel(x, w_in, conv_w, attn_out_g, conv_out_g, w_o, ln1_g, ln1_b,
           w_up, w_down, ln2_g, ln2_b):
    batch, seq, d = x.shape
    assert w_in.shape[0] == DEPTH
    assert seq % MOBA_BLOCK == 0 and d == ATTN_WIDTH + CONV_WIDTH
    n_in = w_in.shape[-1]
    col_scale = jnp.where(jnp.arange(n_in) < ATTN_WIDTH, HEAD_DIM ** -0.5, 1.0
                          ).astype(F32).reshape(1, n_in)
    x2 = x.reshape(batch * seq, d)
    for l in range(DEPTH):
        proj = _proj(x2.astype(BF16), w_in[l].astype(BF16), col_scale)
        attn = _attention(proj, batch, seq)
        h, hb = _mix(attn, proj, x2, conv_w[l],
                     attn_out_g[l].reshape(1, -1), conv_out_g[l].reshape(1, -1),
                     w_o[l].astype(BF16), ln1_g[l].reshape(1, -1), ln1_b[l].reshape(1, -1),
                     seq)
        x2 = _ffn(h, hb, w_up[l].astype(BF16), w_down[l].astype(BF16),
                  ln2_g[l].reshape(1, -1), ln2_b[l].reshape(1, -1))
    return x2.reshape(batch, seq, d)
```

```python
import functools

import jax
import jax.numpy as jnp
from jax import lax
from jax.experimental import pallas as pl
from jax.experimental.pallas import tpu as pltpu

ATTN_WIDTH = 1024
CONV_WIDTH = 1024
HEAD_DIM = 128
N_HEADS = ATTN_WIDTH // HEAD_DIM
CONV_K = 3
MOBA_BLOCK = 256
MOBA_TOPK = 3
LN_EPS = 1e-5
RMS_EPS = 1e-6
DEPTH = 1
DEEPNORM_ALPHA = (2.0 * DEPTH) ** 0.25

SUBLANES = 8
VMEM_LIMIT_BYTES = 56 * 1024 * 1024

F32 = jnp.float32
BF16 = jnp.bfloat16


def _compiler_params(semantics):
    return pltpu.CompilerParams(dimension_semantics=semantics,
                                vmem_limit_bytes=VMEM_LIMIT_BYTES)


def _proj_kernel(x_ref, w_ref, s_ref, o_ref, xb_ref):
    @pl.when(pl.program_id(1) == 0)
    def _():
        xb_ref[...] = x_ref[...].astype(xb_ref.dtype)

    acc = jnp.dot(xb_ref[...], w_ref[...], preferred_element_type=F32)
    o_ref[...] = (acc * s_ref[...]).astype(o_ref.dtype)


def _proj(x2, wb, col_scale, tm=1024, tn=1024):
    t, d = x2.shape
    n = wb.shape[1]
    return pl.pallas_call(
        _proj_kernel,
        grid=(t // tm, n // tn),
        in_specs=[pl.BlockSpec((tm, d), lambda i, j: (i, 0)),
                  pl.BlockSpec((d, tn), lambda i, j: (0, j)),
                  pl.BlockSpec((1, tn), lambda i, j: (0, j))],
        out_specs=pl.BlockSpec((tm, tn), lambda i, j: (i, j)),
        out_shape=jax.ShapeDtypeStruct((t, n), BF16),
        scratch_shapes=[pltpu.VMEM((tm, d), BF16)],
        compiler_params=_compiler_params(("parallel", "arbitrary")),
        name="proj",
    )(x2, wb, col_scale)


def _attn_kernel(q_ref, k_ref, v_ref, o_ref, *, seq, n_blocks, top_k):
    blk = MOBA_BLOCK
    k_all = k_ref[...]
    km = jnp.mean(k_all.astype(F32).reshape(n_blocks, blk, HEAD_DIM), axis=1)
    km_hi = km.astype(BF16)
    km_lo = (km - km_hi.astype(F32)).astype(BF16)
    q_all = q_ref[...]
    nt = (((1,), (1,)), ((), ()))
    gate = (lax.dot_general(km_hi, q_all, nt, preferred_element_type=F32)
            + lax.dot_general(km_lo, q_all, nt, preferred_element_type=F32))

    blk_id = lax.broadcasted_iota(jnp.int32, (n_blocks, seq), 0)
    own = lax.broadcasted_iota(jnp.int32, (n_blocks, seq), 1) // blk
    rank = jnp.zeros((n_blocks, seq), jnp.int32)
    for m in range(n_blocks - 1):
        g_m = gate[m:m + 1, :]
        beats = (g_m > gate) | ((g_m == gate) & (m < blk_id))
        rank = rank + jnp.where(beats & (m < own), 1, 0)
    sel = (rank < top_k) & (blk_id < own)

    vt = v_ref[...].astype(F32).T.astype(BF16)

    key_in_blk = lax.broadcasted_iota(jnp.int32, (blk, blk), 0)
    qry_in_blk = lax.broadcasted_iota(jnp.int32, (blk, blk), 1)
    causal = key_in_blk <= qry_in_blk
    neg_inf = jnp.float32(-jnp.inf)

    for i in range(n_blocks):
        lo, hi = i * blk, (i + 1) * blk
        q_i = q_ref[lo:hi, :]
        s = lax.dot_general(k_ref[0:hi, :], q_i, nt,
                            preferred_element_type=F32)
        pieces = [jnp.where(sel[n:n + 1, lo:hi], s[n * blk:(n + 1) * blk, :], neg_inf)
                  for n in range(i)]
        pieces.append(jnp.where(causal, s[lo:hi, :], neg_inf))
        mx = functools.reduce(jnp.maximum, pieces)
        m_row = jnp.max(mx, axis=0, keepdims=True)
        probs = [jnp.exp(p - m_row) for p in pieces]
        denom = jnp.sum(functools.reduce(jnp.add, probs), axis=0, keepdims=True)
        p_all = jnp.concatenate([p.astype(BF16) for p in probs], axis=0)
        o_t = jnp.dot(vt[:, 0:hi], p_all, preferred_element_type=F32)
        o_ref[lo:hi, :] = (o_t / denom).T.astype(o_ref.dtype)


def _attention(proj, batch, seq):
    n_blocks = seq // MOBA_BLOCK
    top_k = max(1, min(MOBA_TOPK, n_blocks - 1))
    kern = functools.partial(_attn_kernel, seq=seq, n_blocks=n_blocks, top_k=top_k)
    head_spec = lambda off: pl.BlockSpec((seq, HEAD_DIM), lambda b, h: (b, off + h))
    return pl.pallas_call(
        kern,
        grid=(batch, N_HEADS),
        in_specs=[head_spec(0), head_spec(N_HEADS), head_spec(2 * N_HEADS)],
        out_specs=pl.BlockSpec((seq, HEAD_DIM), lambda b, h: (b, h)),
        out_shape=jax.ShapeDtypeStruct((batch * seq, ATTN_WIDTH), BF16),
        compiler_params=_compiler_params(("parallel", "arbitrary")),
        name="moba_attn",
    )(proj, proj, proj)


def _rms_norm(v, gain):
    ms = jnp.mean(jnp.square(v), axis=-1, keepdims=True)
    return v * lax.rsqrt(ms + RMS_EPS) * gain


def _layer_norm(v, gain, bias):
    mu = jnp.mean(v, axis=-1, keepdims=True)
    vc = v - mu
    var = jnp.mean(jnp.square(vc), axis=-1, keepdims=True)
    return vc * lax.rsqrt(var + LN_EPS) * gain + bias


def _mix_kernel(attn_ref, b_ref, c_ref, u_ref, ch_ref, uh_ref, x_ref, cw_ref,
                ga_ref, gc_ref, wo_ref, g1_ref, b1_ref, h_ref, hb_ref, *,
                tiles_per_seq, sub):
    i = pl.program_id(0)
    tm, a_w = attn_ref.shape
    not_first = (i % tiles_per_seq != 0).astype(F32)
    zh = ch_ref[...].astype(F32) * uh_ref[...].astype(F32) * not_first

    for s in range(tm // sub):
        rows = slice(s * sub, (s + 1) * sub)
        ra = _rms_norm(attn_ref[rows, :].astype(F32), ga_ref[...])
        z = c_ref[rows, :].astype(F32) * u_ref[rows, :].astype(F32)
        if s > 0:
            halo = slice(s * sub - SUBLANES, s * sub)
            zh = c_ref[halo, :].astype(F32) * u_ref[halo, :].astype(F32)
        row = lax.broadcasted_iota(jnp.int32, z.shape, 0)
        z1 = jnp.where(row == 0, zh[SUBLANES - 1:SUBLANES, :], pltpu.roll(z, 1, 0))
        z2 = jnp.where(row == 0, zh[SUBLANES - 2:SUBLANES - 1, :],
                       jnp.where(row == 1, zh[SUBLANES - 1:SUBLANES, :], pltpu.roll(z, 2, 0)))
        y = cw_ref[0:1, :] * z2 + cw_ref[1:2, :] * z1 + cw_ref[2:3, :] * z
        rc = _rms_norm(b_ref[rows, :].astype(F32) * y, gc_ref[...])
        m = (jnp.dot(ra.astype(BF16), wo_ref[0:a_w, :], preferred_element_type=F32)
             + jnp.dot(rc.astype(BF16), wo_ref[a_w:, :], preferred_element_type=F32))
        h = _layer_norm(DEEPNORM_ALPHA * x_ref[rows, :] + m, g1_ref[...], b1_ref[...])
        h_ref[rows, :] = h
        hb_ref[rows, :] = h.astype(hb_ref.dtype)


def _mix(attn, proj, x2, conv_w, ga, gc, wo_b, g1, b1, seq, tm=512, sub=256):
    t, d = x2.shape
    a_w, c_w = ATTN_WIDTH, CONV_WIDTH
    tiles_per_seq = seq // tm
    halo_blocks = tm // SUBLANES
    cb = 3 * a_w // c_w
    row_blk = lambda col: pl.BlockSpec((tm, c_w), lambda i: (i, col))
    halo = lambda col: pl.BlockSpec(
        (SUBLANES, c_w), lambda i: (jnp.maximum(i * halo_blocks - 1, 0), col))
    vec = lambda w: pl.BlockSpec((1, w), lambda i: (0, 0))
    kern = functools.partial(_mix_kernel, tiles_per_seq=tiles_per_seq, sub=sub)
    return pl.pallas_call(
        kern,
        grid=(t // tm,),
        in_specs=[pl.BlockSpec((tm, a_w), lambda i: (i, 0)),
                  row_blk(cb), row_blk(cb + 1), row_blk(cb + 2),
                  halo(cb + 1), halo(cb + 2),
                  pl.BlockSpec((tm, d), lambda i: (i, 0)),
                  pl.BlockSpec((CONV_K, c_w), lambda i: (0, 0)),
                  vec(a_w), vec(c_w),
                  pl.BlockSpec((d, d), lambda i: (0, 0)),
                  vec(d), vec(d)],
        out_specs=[pl.BlockSpec((tm, d), lambda i: (i, 0)),
                   pl.BlockSpec((tm, d), lambda i: (i, 0))],
        out_shape=[jax.ShapeDtypeStruct((t, d), F32),
                   jax.ShapeDtypeStruct((t, d), BF16)],
        compiler_params=_compiler_params(("parallel",)),
        name="mix_ln1",
    )(attn, proj, proj, proj, proj, proj, x2, conv_w, ga, gc, wo_b, g1, b1)


def _ffn_kernel(h_ref, hb_ref, wu_ref, wd_ref, g_ref, b_ref, o_ref):
    f = pl.program_id(1)

    @pl.when(f == 0)
    def _():
        o_ref[...] = jnp.zeros_like(o_ref)

    up = jnp.dot(hb_ref[...], wu_ref[...], preferred_element_type=F32)
    act = jnp.square(jnp.maximum(up, 0.0)).astype(BF16)
    o_ref[...] += jnp.dot(act, wd_ref[...], preferred_element_type=F32)

    @pl.when(f == pl.num_programs(1) - 1)
    def _():
        o_ref[...] = _layer_norm(DEEPNORM_ALPHA * h_ref[...] + o_ref[...],
                                 g_ref[...], b_ref[...])


def _ffn(h, hb, wu_b, wd_b, g2, b2, tm=512, tf=1024):
    t, d = h.shape
    d_ff = wu_b.shape[1]
    vec = pl.BlockSpec((1, d), lambda i, f: (0, 0))
    return pl.pallas_call(
        _ffn_kernel,
        grid=(t // tm, d_ff // tf),
        in_specs=[pl.BlockSpec((tm, d), lambda i, f: (i, 0)),
                  pl.BlockSpec((tm, d), lambda i, f: (i, 0)),
                  pl.BlockSpec((d, tf), lambda i, f: (0, f)),
                  pl.BlockSpec((tf, d), lambda i, f: (f, 0)),
                  vec, vec],
        out_specs=pl.BlockSpec((tm, d), lambda i, f: (i, 0)),
        out_shape=jax.ShapeDtypeStruct((t, d), F32),
        compiler_params=_compiler_params(("parallel", "arbitrary")),
        name="ffn_ln2",
    )(h, hb, wu_b, wd_b, g2, b2)


def kernel(x, w_in, conv_w, attn_out_g, conv_out_g, w_o, ln1_g, ln1_b,
           w_up, w_down, ln2_g, ln2_b):
    batch, seq, d = x.shape
    assert w_in.shape[0] == DEPTH
    assert seq % MOBA_BLOCK == 0 and d == ATTN_WIDTH + CONV_WIDTH
    n_in = w_in.shape[-1]
    col_scale = jnp.where(jnp.arange(n_in) < ATTN_WIDTH, HEAD_DIM ** -0.5, 1.0
                          ).astype(F32).reshape(1, n_in)
    x2 = x.reshape(batch * seq, d)
    for l in range(DEPTH):
        proj = _proj(x2, w_in[l].astype(BF16), col_scale)
        attn = _attention(proj, batch, seq)
        h, hb = _mix(attn, proj, x2, conv_w[l],
                     attn_out_g[l].reshape(1, -1), conv_out_g[l].reshape(1, -1),
                     w_o[l].astype(BF16), ln1_g[l].reshape(1, -1), ln1_b[l].reshape(1, -1),
                     seq)
        x2 = _ffn(h, hb, w_up[l].astype(BF16), w_down[l].astype(BF16),
                  ln2_g[l].reshape(1, -1), ln2_b[l].reshape(1, -1))
    return x2.reshape(batch, seq, d)
```

```python
import functools
import math

import jax
import jax.numpy as jnp
from jax import lax
from jax.experimental import pallas as pl
from jax.experimental.pallas import tpu as pltpu

ATTN_WIDTH = 1024
CONV_WIDTH = 1024
HEAD_DIM = 128
N_HEADS = ATTN_WIDTH // HEAD_DIM
CONV_K = 3
MOBA_BLOCK = 256
MOBA_TOPK = 3
LN_EPS = 1e-5
RMS_EPS = 1e-6
DEPTH = 1
DEEPNORM_ALPHA = (2.0 * DEPTH) ** 0.25
LOG2_E = math.log2(math.e)

SUBLANES = 8
VMEM_LIMIT_BYTES = 56 * 1024 * 1024

F32 = jnp.float32
BF16 = jnp.bfloat16


def _compiler_params(semantics):
    return pltpu.CompilerParams(dimension_semantics=semantics,
                                vmem_limit_bytes=VMEM_LIMIT_BYTES)


def _proj_kernel(x_ref, w_ref, s_ref, o_ref, xb_ref):
    @pl.when(pl.program_id(1) == 0)
    def _():
        xb_ref[...] = x_ref[...].astype(xb_ref.dtype)

    acc = jnp.dot(xb_ref[...], w_ref[...], preferred_element_type=F32)
    o_ref[...] = (acc * s_ref[...]).astype(o_ref.dtype)


def _proj(x2, wb, col_scale, tm=1024, tn=1024):
    t, d = x2.shape
    n = wb.shape[1]
    return pl.pallas_call(
        _proj_kernel,
        grid=(t // tm, n // tn),
        in_specs=[pl.BlockSpec((tm, d), lambda i, j: (i, 0)),
                  pl.BlockSpec((d, tn), lambda i, j: (0, j)),
                  pl.BlockSpec((1, tn), lambda i, j: (0, j))],
        out_specs=pl.BlockSpec((tm, tn), lambda i, j: (i, j)),
        out_shape=jax.ShapeDtypeStruct((t, n), BF16),
        scratch_shapes=[pltpu.VMEM((tm, d), BF16)],
        compiler_params=_compiler_params(("parallel", "arbitrary")),
        name="proj",
    )(x2, wb, col_scale)


def _attn_kernel(q_ref, k_ref, v_ref, o_ref, s0_ref, s1_ref, p0_ref, p1_ref, vt_ref, *,
                 seq, n_blocks, top_k):
    blk = MOBA_BLOCK
    nt = (((1,), (1,)), ((), ()))

    def block_selection():
        km = jnp.mean(k_ref[...].astype(F32).reshape(n_blocks, blk, HEAD_DIM), axis=1)
        km_hi = km.astype(BF16)
        km_lo = (km - km_hi.astype(F32)).astype(BF16)
        q_all = q_ref[...]
        gate = (lax.dot_general(km_hi, q_all, nt, preferred_element_type=F32)
                + lax.dot_general(km_lo, q_all, nt, preferred_element_type=F32))
        blk_id = lax.broadcasted_iota(jnp.int32, (n_blocks, seq), 0)
        own = lax.broadcasted_iota(jnp.int32, (n_blocks, seq), 1) // blk
        rank = jnp.zeros((n_blocks, seq), jnp.int32)
        for m in range(n_blocks - 1):
            g_m = gate[m:m + 1, :]
            beats = (g_m > gate) | ((g_m == gate) & (m < blk_id))
            rank = rank + jnp.where(beats & (m < own), 1, 0)
        return (rank < top_k) & (blk_id < own)

    key_in_blk = lax.broadcasted_iota(jnp.int32, (blk, blk), 0)
    qry_in_blk = lax.broadcasted_iota(jnp.int32, (blk, blk), 1)
    causal = key_in_blk <= qry_in_blk
    neg_inf = jnp.float32(-jnp.inf)

    s_scr, p_scr = (s0_ref, s1_ref), (p0_ref, p1_ref)
    fold = lambda a: a.reshape(blk // SUBLANES, SUBLANES, blk)

    def masked_scores(i, n, slot):
        s = lax.dot_general(k_ref[n * blk:(n + 1) * blk, :], q_ref[i * blk:(i + 1) * blk, :], nt,
                            preferred_element_type=F32)
        if n < i:
            mask = sel[n:n + 1, i * blk:(i + 1) * blk]
        else:
            mask = causal
        s = jnp.where(mask, s, neg_inf)
        s_scr[slot][n * blk:(n + 1) * blk, :] = s
        return jnp.max(fold(s), axis=0)

    mx8 = masked_scores(0, 0, 0)
    sel = block_selection()
    for i in range(n_blocks):
        lo, hi, slot = i * blk, (i + 1) * blk, i % 2
        vt_ref[:, lo:hi] = v_ref[lo:hi, :].astype(F32).T.astype(BF16)
        m_row = jnp.max(mx8, axis=0, keepdims=True)
        nxt = i + 1 < n_blocks
        mx8_next = None
        l8 = jnp.zeros((SUBLANES, blk), F32)
        for n in range(i + 1):
            if nxt:
                m8 = masked_scores(i + 1, n, 1 - slot)
                mx8_next = m8 if mx8_next is None else jnp.maximum(mx8_next, m8)
            p = jnp.exp2(s_scr[slot][n * blk:(n + 1) * blk, :] - m_row)
            l8 = l8 + jnp.sum(fold(p), axis=0)
            p_scr[slot][n * blk:(n + 1) * blk, :] = p.astype(BF16)
        if nxt:
            mx8 = jnp.maximum(mx8_next, masked_scores(i + 1, i + 1, 1 - slot))
        o_t = jnp.dot(vt_ref[:, 0:hi], p_scr[slot][0:hi, :],
                      preferred_element_type=F32)
        denom = jnp.sum(l8, axis=0, keepdims=True)
        o_ref[lo:hi, :] = (o_t / denom).T.astype(o_ref.dtype)


def _attention(proj, batch, seq):
    n_blocks = seq // MOBA_BLOCK
    top_k = max(1, min(MOBA_TOPK, n_blocks - 1))
    kern = functools.partial(_attn_kernel, seq=seq, n_blocks=n_blocks, top_k=top_k)
    head_spec = lambda off: pl.BlockSpec((seq, HEAD_DIM), lambda b, h: (b, off + h))
    return pl.pallas_call(
        kern,
        grid=(batch, N_HEADS),
        in_specs=[head_spec(0), head_spec(N_HEADS), head_spec(2 * N_HEADS)],
        out_specs=pl.BlockSpec((seq, HEAD_DIM), lambda b, h: (b, h)),
        out_shape=jax.ShapeDtypeStruct((batch * seq, ATTN_WIDTH), BF16),
        scratch_shapes=[pltpu.VMEM((seq, MOBA_BLOCK), F32), pltpu.VMEM((seq, MOBA_BLOCK), F32),
                        pltpu.VMEM((seq, MOBA_BLOCK), BF16), pltpu.VMEM((seq, MOBA_BLOCK), BF16),
                        pltpu.VMEM((HEAD_DIM, seq), BF16)],
        compiler_params=_compiler_params(("parallel", "arbitrary")),
        name="moba_attn",
    )(proj, proj, proj)


def _rms_norm(v, gain):
    ms = jnp.mean(jnp.square(v), axis=-1, keepdims=True)
    return v * lax.rsqrt(ms + RMS_EPS) * gain


def _layer_norm(v, gain, bias):
    mu = jnp.mean(v, axis=-1, keepdims=True)
    vc = v - mu
    var = jnp.mean(jnp.square(vc), axis=-1, keepdims=True)
    return vc * lax.rsqrt(var + LN_EPS) * gain + bias


def _mix_kernel(attn_ref, b_ref, c_ref, u_ref, ch_ref, uh_ref, x_ref, cw_ref,
                ga_ref, gc_ref, wo_ref, g1_ref, b1_ref, h_ref, hb_ref, *,
                tiles_per_seq, sub):
    i = pl.program_id(0)
    tm, a_w = attn_ref.shape
    not_first = (i % tiles_per_seq != 0).astype(F32)
    zh = ch_ref[...].astype(F32) * uh_ref[...].astype(F32) * not_first

    for s in range(tm // sub):
        rows = slice(s * sub, (s + 1) * sub)
        ra = _rms_norm(attn_ref[rows, :].astype(F32), ga_ref[...])
        z = c_ref[rows, :].astype(F32) * u_ref[rows, :].astype(F32)
        if s > 0:
            halo = slice(s * sub - SUBLANES, s * sub)
            zh = c_ref[halo, :].astype(F32) * u_ref[halo, :].astype(F32)
        row = lax.broadcasted_iota(jnp.int32, z.shape, 0)
        z1 = jnp.where(row == 0, zh[SUBLANES - 1:SUBLANES, :], pltpu.roll(z, 1, 0))
        z2 = jnp.where(row == 0, zh[SUBLANES - 2:SUBLANES - 1, :],
                       jnp.where(row == 1, zh[SUBLANES - 1:SUBLANES, :], pltpu.roll(z, 2, 0)))
        y = cw_ref[0:1, :] * z2 + cw_ref[1:2, :] * z1 + cw_ref[2:3, :] * z
        rc = _rms_norm(b_ref[rows, :].astype(F32) * y, gc_ref[...])
        m = (jnp.dot(ra.astype(BF16), wo_ref[0:a_w, :], preferred_element_type=F32)
             + jnp.dot(rc.astype(BF16), wo_ref[a_w:, :], preferred_element_type=F32))
        h = _layer_norm(DEEPNORM_ALPHA * x_ref[rows, :] + m, g1_ref[...], b1_ref[...])
        h_ref[rows, :] = h
        hb_ref[rows, :] = h.astype(hb_ref.dtype)


def _mix(attn, proj, x2, conv_w, ga, gc, wo_b, g1, b1, seq, tm=512, sub=256):
    t, d = x2.shape
    a_w, c_w = ATTN_WIDTH, CONV_WIDTH
    tiles_per_seq = seq // tm
    halo_blocks = tm // SUBLANES
    cb = 3 * a_w // c_w
    row_blk = lambda col: pl.BlockSpec((tm, c_w), lambda i: (i, col))
    halo = lambda col: pl.BlockSpec(
        (SUBLANES, c_w), lambda i: (jnp.maximum(i * halo_blocks - 1, 0), col))
    vec = lambda w: pl.BlockSpec((1, w), lambda i: (0, 0))
    kern = functools.partial(_mix_kernel, tiles_per_seq=tiles_per_seq, sub=sub)
    return pl.pallas_call(
        kern,
        grid=(t // tm,),
        in_specs=[pl.BlockSpec((tm, a_w), lambda i: (i, 0)),
                  row_blk(cb), row_blk(cb + 1), row_blk(cb + 2),
                  halo(cb + 1), halo(cb + 2),
                  pl.BlockSpec((tm, d), lambda i: (i, 0)),
                  pl.BlockSpec((CONV_K, c_w), lambda i: (0, 0)),
                  vec(a_w), vec(c_w),
                  pl.BlockSpec((d, d), lambda i: (0, 0)),
                  vec(d), vec(d)],
        out_specs=[pl.BlockSpec((tm, d), lambda i: (i, 0)),
                   pl.BlockSpec((tm, d), lambda i: (i, 0))],
        out_shape=[jax.ShapeDtypeStruct((t, d), F32),
                   jax.ShapeDtypeStruct((t, d), BF16)],
        compiler_params=_compiler_params(("parallel",)),
        name="mix_ln1",
    )(attn, proj, proj, proj, proj, proj, x2, conv_w, ga, gc, wo_b, g1, b1)


def _ffn_kernel(h_ref, hb_ref, wu_ref, wd_ref, g_ref, b_ref, o_ref):
    f = pl.program_id(1)

    @pl.when(f == 0)
    def _():
        o_ref[...] = jnp.zeros_like(o_ref)

    up = jnp.dot(hb_ref[...], wu_ref[...], preferred_element_type=F32)
    act = jnp.square(jnp.maximum(up, 0.0)).astype(BF16)
    o_ref[...] += jnp.dot(act, wd_ref[...], preferred_element_type=F32)

    @pl.when(f == pl.num_programs(1) - 1)
    def _():
        o_ref[...] = _layer_norm(DEEPNORM_ALPHA * h_ref[...] + o_ref[...],
                                 g_ref[...], b_ref[...])


def _ffn(h, hb, wu_b, wd_b, g2, b2, tm=512, tf=1024):
    t, d = h.shape
    d_ff = wu_b.shape[1]
    vec = pl.BlockSpec((1, d), lambda i, f: (0, 0))
    return pl.pallas_call(
        _ffn_kernel,
        grid=(t // tm, d_ff // tf),
        in_specs=[pl.BlockSpec((tm, d), lambda i, f: (i, 0)),
                  pl.BlockSpec((tm, d), lambda i, f: (i, 0)),
                  pl.BlockSpec((d, tf), lambda i, f: (0, f)),
                  pl.BlockSpec((tf, d), lambda i, f: (f, 0)),
                  vec, vec],
        out_specs=pl.BlockSpec((tm, d), lambda i, f: (i, 0)),
        out_shape=jax.ShapeDtypeStruct((t, d), F32),
        compiler_params=_compiler_params(("parallel", "arbitrary")),
        name="ffn_ln2",
    )(h, hb, wu_b, wd_b, g2, b2)


def kernel(x, w_in, conv_w, attn_out_g, conv_out_g, w_o, ln1_g, ln1_b,
           w_up, w_down, ln2_g, ln2_b):
    batch, seq, d = x.shape
    assert w_in.shape[0] == DEPTH
    assert seq % MOBA_BLOCK == 0 and d == ATTN_WIDTH + CONV_WIDTH
    n_in = w_in.shape[-1]
    col_scale = jnp.where(jnp.arange(n_in) < ATTN_WIDTH, HEAD_DIM ** -0.5 * LOG2_E, 1.0
                          ).astype(F32).reshape(1, n_in)
    x2 = x.reshape(batch * seq, d)
    for l in range(DEPTH):
        proj = _proj(x2, w_in[l].astype(BF16), col_scale)
        attn = _attention(proj, batch, seq)
        h, hb = _mix(attn, proj, x2, conv_w[l],
                     attn_out_g[l].reshape(1, -1), conv_out_g[l].reshape(1, -1),
                     w_o[l].astype(BF16), ln1_g[l].reshape(1, -1), ln1_b[l].reshape(1, -1),
                     seq)
        x2 = _ffn(h, hb, w_up[l].astype(BF16), w_down[l].astype(BF16),
                  ln2_g[l].reshape(1, -1), ln2_b[l].reshape(1, -1))
    return x2.reshape(batch, seq, d)
```
